```python
import jax, jax.numpy as jnp
from jax import lax
import numpy as np

D_MODEL = 2048
BATCH = 1
SEQ = 8192
DEPTH = 4

N_MIXERS = 2
N_CONV_LAYERS = (DEPTH + N_MIXERS - 1) // N_MIXERS
N_ATTN_LAYERS = DEPTH // N_MIXERS
CONV_WIDTH = 3
HEAD_DIM = 128
N_HEADS = D_MODEL // HEAD_DIM
D_ATTN = N_HEADS * HEAD_DIM
DILATED_GROUPS = ((128, 1), (512, 4), (2048, 16))
N_GROUPS = len(DILATED_GROUPS)
BLOCK = 128
D_FF = 5632
RMS_EPS = 1e-5
NEG_INF = -1e30

kernel_name = "hybrid_shortconv_dilated_swa_convffn"


def rmsnorm(x, g):
    xf = x.astype(jnp.float32)
    y = xf * lax.rsqrt(jnp.mean(xf * xf, axis=-1, keepdims=True) + RMS_EPS)
    return (y * g.astype(jnp.float32)).astype(x.dtype)


def causal_dwconv(x, w):
    S = x.shape[1]
    xp = jnp.pad(x, ((0, 0), (CONV_WIDTH - 1, 0), (0, 0)))
    y = w[0] * xp[:, 0:S]
    for j in range(1, CONV_WIDTH):
        y = y + w[j] * xp[:, j:j + S]
    return y


def short_conv_mixer(x, w_in, w_dw, w_out):
    b, c, h = jnp.split(x @ w_in, 3, axis=-1)
    return (b * causal_dwconv(c * h, w_dw)) @ w_out


def alibi_slopes():
    return 2.0 ** (-8.0 * jnp.arange(1, N_HEADS + 1, dtype=jnp.float32) / N_HEADS)


def dilated_window_attention(q, k, v, window, dil, slopes):
    B, S, H, Dh = q.shape
    n_back = window // dil
    assert n_back <= BLOCK
    L = S // dil
    Lp = -(-L // BLOCK) * BLOCK
    nb = Lp // BLOCK

    def to_blocks(a):
        a = a.reshape(B, L, dil, H, Dh).transpose(0, 2, 1, 3, 4)
        a = jnp.pad(a, ((0, 0), (0, 0), (0, Lp - L), (0, 0), (0, 0)))
        return a.reshape(B, dil, nb, BLOCK, H, Dh)

    def with_prev(a):
        prev = jnp.pad(a, ((0, 0), (0, 0), (1, 0), (0, 0), (0, 0), (0, 0)))[:, :, :-1]
        return jnp.concatenate([prev, a], axis=3)

    qb = to_blocks(q)
    kk = with_prev(to_blocks(k))
    vv = with_prev(to_blocks(v))

    s = jnp.einsum('brnqhd,brnkhd->brnhqk', qb.astype(jnp.float32), kk.astype(jnp.float32)) * (HEAD_DIM ** -0.5)
    qi = jnp.arange(BLOCK)[:, None]
    ki = jnp.arange(2 * BLOCK)[None, :]
    rel = BLOCK + qi - ki
    key_pos = jnp.arange(nb)[:, None, None] * BLOCK + ki[None] - BLOCK
    valid = (rel >= 0)[None] & (rel <= n_back)[None] & (key_pos >= 0)
    bias = -slopes[:, None, None] * (dil * rel).astype(jnp.float32)[None]
    s = jnp.where(valid[None, None, :, None], s + bias[None, None, None], NEG_INF)

    m = jnp.max(s, axis=-1, keepdims=True)
    p = jnp.exp(s - m)
    den = jnp.sum(p, axis=-1, keepdims=True)
    o = jnp.einsum('brnhqk,brnkhd->brnqhd', p / den, vv.astype(jnp.float32))
    lse = (m + jnp.log(den))[..., 0]

    o = o.reshape(B, dil, Lp, H, Dh)[:, :, :L].transpose(0, 2, 1, 3, 4).reshape(B, S, H, Dh)
    lse = lse.transpose(0, 1, 2, 4, 3).reshape(B, dil, Lp, H)[:, :, :L].transpose(0, 2, 1, 3).reshape(B, S, H)
    return o, lse


def dilated_attention_mixer(x, w_qkv, w_o):
    B, S, _ = x.shape
    qkv = (x @ w_qkv).reshape(B, S, N_GROUPS, 3, N_HEADS, HEAD_DIM)
    slopes = alibi_slopes()
    outs, lses = [], []
    for g, (window, dil) in enumerate(DILATED_GROUPS):
        o, lse = dilated_window_attention(qkv[:, :, g, 0], qkv[:, :, g, 1], qkv[:, :, g, 2], window, dil, slopes)
        outs.append(o)
        lses.append(lse)
    alpha = jax.nn.softmax(jnp.stack(lses, axis=0), axis=0)
    o = jnp.sum(alpha[..., None] * jnp.stack(outs, axis=0), axis=0)
    return o.reshape(B, S, D_ATTN).astype(x.dtype) @ w_o


def conv_ffn(x, w_up, w_dw, w_down):
    h = causal_dwconv(x @ w_up, w_dw)
    g, u = jnp.split(h, 2, axis=-1)
    return (jax.nn.silu(g) * u) @ w_down


def setup_inputs(seed: int = 0) -> dict:
    key = jax.random.key(seed)
    ks = jax.random.split(key, 13)
    f32 = jnp.float32
    nrm = lambda k, shape, scale: jax.random.normal(k, shape, f32) * scale
    return {
        "x": nrm(ks[0], (BATCH, SEQ, D_MODEL), 1.0),
        "norm_mix_g": 1.0 + nrm(ks[1], (DEPTH, D_MODEL), 0.01),
        "norm_ffn_g": 1.0 + nrm(ks[2], (DEPTH, D_MODEL), 0.01),
        "conv_w_in": nrm(ks[3], (N_CONV_LAYERS, D_MODEL, 3 * D_MODEL), D_MODEL ** -0.5),
        "conv_w_dw": nrm(ks[4], (N_CONV_LAYERS, CONV_WIDTH, D_MODEL), 0.5),
        "conv_w_out": nrm(ks[5], (N_CONV_LAYERS, D_MODEL, D_MODEL), D_MODEL ** -0.5),
        "attn_w_qkv": nrm(ks[6], (N_ATTN_LAYERS, D_MODEL, N_GROUPS * 3 * D_ATTN), D_MODEL ** -0.5),
        "attn_w_o": nrm(ks[7], (N_ATTN_LAYERS, D_ATTN, D_MODEL), D_ATTN ** -0.5),
        "ffn_w_up": nrm(ks[8], (DEPTH, D_MODEL, 2 * D_FF), D_MODEL ** -0.5),
        "ffn_w_dw": nrm(ks[9], (DEPTH, CONV_WIDTH, 2 * D_FF), 0.5),
        "ffn_w_down": nrm(ks[10], (DEPTH, D_FF, D_MODEL), D_FF ** -0.5),
        "final_norm_g": 1.0 + nrm(ks[11], (D_MODEL,), 0.01),
    }


def reference(x, norm_mix_g, norm_ffn_g, conv_w_in, conv_w_dw, conv_w_out,
              attn_w_qkv, attn_w_o, ffn_w_up, ffn_w_dw, ffn_w_down, final_norm_g):
    for i in range(DEPTH):
        h = rmsnorm(x, norm_mix_g[i])
        j = i // N_MIXERS
        if i % N_MIXERS == 0:
            x = x + short_conv_mixer(h, conv_w_in[j], conv_w_dw[j], conv_w_out[j])
        else:
            x = x + dilated_attention_mixer(h, attn_w_qkv[j], attn_w_o[j])
        x = x + conv_ffn(rmsnorm(x, norm_ffn_g[i]), ffn_w_up[i], ffn_w_dw[i], ffn_w_down[i])
    return rmsnorm(x, final_norm_g)
```

```python
import functools

import numpy as np
import jax
import jax.numpy as jnp
from jax import lax
from jax.experimental import pallas as pl
from jax.experimental.pallas import tpu as pltpu

D_MODEL = 2048
SEQ = 8192
DEPTH = 4
CONV_WIDTH = 3
HEAD_DIM = 128
N_HEADS = D_MODEL // HEAD_DIM
DILATED_GROUPS = ((128, 1), (512, 4), (2048, 16))
N_GROUPS = len(DILATED_GROUPS)
BLOCK = 128
D_FF = 5632
RMS_EPS = 1e-5
NEG_INF = -1e30

SUBLANES = 8
LANES = 128
VMEM_LIMIT = 56 * 1024 * 1024

F32 = jnp.float32
BF16 = jnp.bfloat16


def _cparams(n_axes):
    return pltpu.CompilerParams(
        dimension_semantics=("arbitrary",) * n_axes,
        vmem_limit_bytes=VMEM_LIMIT)


def _rms_tile(x, g):
    ms = jnp.mean(x * x, axis=-1, keepdims=True)
    return x * lax.rsqrt(ms + RMS_EPS) * g


def _dot(a, b):
    return jnp.dot(a, b, preferred_element_type=F32)


def _causal_conv3(u, buf_ref, carry, dw):
    bm = u.shape[0]
    buf_ref[0:SUBLANES, :] = carry
    buf_ref[SUBLANES:SUBLANES + bm, :] = u
    u1 = buf_ref[SUBLANES - 1:SUBLANES - 1 + bm, :]
    u2 = buf_ref[SUBLANES - 2:SUBLANES - 2 + bm, :]
    return dw[0:1, :] * u2 + dw[1:2, :] * u1 + dw[2:3, :] * u


def _conv_up_body(x_ref, g_ref, wb_ref, wc_ref, wh_ref, dw_ref, y_ref,
                  hn_ref, carry_ref, buf_ref):
    m = pl.program_id(0)
    n = pl.program_id(1)
    bm = x_ref.shape[0]

    @pl.when(n == 0)
    def _():
        hn_ref[...] = _rms_tile(x_ref[...], g_ref[...]).astype(BF16)

    @pl.when(m == 0)
    def _():
        carry_ref[n] = jnp.zeros(carry_ref.shape[1:], F32)

    hb = hn_ref[...]
    b = _dot(hb, wb_ref[...])
    u = _dot(hb, wc_ref[...]) * _dot(hb, wh_ref[...])
    conv = _causal_conv3(u, buf_ref, carry_ref[n], dw_ref[...])
    carry_ref[n] = buf_ref[bm:bm + SUBLANES, :]
    y_ref[...] = (b * conv).astype(BF16)


def _conv_up(x, g, w_in, w_dw, *, bm, bn):
    nt = D_MODEL // bn
    return pl.pallas_call(
        _conv_up_body,
        grid=(SEQ // bm, nt),
        in_specs=[
            pl.BlockSpec((bm, D_MODEL), lambda m, n: (m, 0)),
            pl.BlockSpec((1, D_MODEL), lambda m, n: (0, 0)),
            pl.BlockSpec((D_MODEL, bn), lambda m, n: (0, n)),
            pl.BlockSpec((D_MODEL, bn), lambda m, n: (0, nt + n)),
            pl.BlockSpec((D_MODEL, bn), lambda m, n: (0, 2 * nt + n)),
            pl.BlockSpec((CONV_WIDTH, bn), lambda m, n: (0, n)),
        ],
        out_specs=pl.BlockSpec((bm, bn), lambda m, n: (m, n)),
        out_shape=jax.ShapeDtypeStruct((SEQ, D_MODEL), BF16),
        scratch_shapes=[
            pltpu.VMEM((bm, D_MODEL), BF16),
            pltpu.VMEM((nt, SUBLANES, bn), F32),
            pltpu.VMEM((bm + SUBLANES, bn), F32),
        ],
        compiler_params=_cparams(2),
        name="conv_up",
    )(x, g, w_in, w_in, w_in, w_dw)


def _ffn_up_body(x_ref, g_ref, wg_ref, wu_ref, dwg_ref, dwu_ref, a_ref,
                 hn_ref, carry_ref, gbuf_ref, ubuf_ref):
    m = pl.program_id(0)
    n = pl.program_id(1)
    bm = x_ref.shape[0]

    @pl.when(n == 0)
    def _():
        hn_ref[...] = _rms_tile(x_ref[...], g_ref[...]).astype(BF16)

    @pl.when(m == 0)
    def _():
        carry_ref[n] = jnp.zeros(carry_ref.shape[1:], F32)

    hb = hn_ref[...]
    gz = _dot(hb, wg_ref[...])
    cg = _causal_conv3(gz, gbuf_ref, carry_ref[n, 0], dwg_ref[...])
    carry_ref[n, 0] = gbuf_ref[bm:bm + SUBLANES, :]
    uz = _dot(hb, wu_ref[...])
    cu = _causal_conv3(uz, ubuf_ref, carry_ref[n, 1], dwu_ref[...])
    carry_ref[n, 1] = ubuf_ref[bm:bm + SUBLANES, :]
    act = cg / (1.0 + jnp.exp(-cg)) * cu
    a_ref[...] = act.astype(BF16)


def _ffn_up(x, g, w_up, w_dw, *, bm, bn):
    nt = D_FF // bn
    return pl.pallas_call(
        _ffn_up_body,
        grid=(SEQ // bm, nt),
        in_specs=[
            pl.BlockSpec((bm, D_MODEL), lambda m, n: (m, 0)),
            pl.BlockSpec((1, D_MODEL), lambda m, n: (0, 0)),
            pl.BlockSpec((D_MODEL, bn), lambda m, n: (0, n)),
            pl.BlockSpec((D_MODEL, bn), lambda m, n: (0, nt + n)),
            pl.BlockSpec((CONV_WIDTH, bn), lambda m, n: (0, n)),
            pl.BlockSpec((CONV_WIDTH, bn), lambda m, n: (0, nt + n)),
        ],
        out_specs=pl.BlockSpec((bm, bn), lambda m, n: (m, n)),
        out_shape=jax.ShapeDtypeStruct((SEQ, D_FF), BF16),
        scratch_shapes=[
            pltpu.VMEM((bm, D_MODEL), BF16),
            pltpu.VMEM((nt, 2, SUBLANES, bn), F32),
            pltpu.VMEM((bm + SUBLANES, bn), F32),
            pltpu.VMEM((bm + SUBLANES, bn), F32),
        ],
        compiler_params=_cparams(2),
        name="ffn_up",
    )(x, g, w_up, w_up, w_dw, w_dw)


def _qkv_body(x_ref, g_ref, w_ref, o_ref, hn_ref, *, bn):
    n = pl.program_id(1)

    @pl.when(n == 0)
    def _():
        hn_ref[...] = _rms_tile(x_ref[...], g_ref[...]).astype(BF16)

    is_q = ((n * bn) // D_MODEL) % 3 == 0
    scale = jnp.where(is_q, HEAD_DIM ** -0.5, 1.0).astype(F32)
    o_ref[...] = (_dot(hn_ref[...], w_ref[...]) * scale).astype(BF16)


def _qkv_proj(x, g, w_qkv, *, bm, bn):
    n_out = w_qkv.shape[1]
    return pl.pallas_call(
        functools.partial(_qkv_body, bn=bn),
        grid=(SEQ // bm, n_out // bn),
        in_specs=[
            pl.BlockSpec((bm, D_MODEL), lambda m, n: (m, 0)),
            pl.BlockSpec((1, D_MODEL), lambda m, n: (0, 0)),
            pl.BlockSpec((D_MODEL, bn), lambda m, n: (0, n)),
        ],
        out_specs=pl.BlockSpec((bm, bn), lambda m, n: (m, n)),
        out_shape=jax.ShapeDtypeStruct((SEQ, n_out), BF16),
        scratch_shapes=[pltpu.VMEM((bm, D_MODEL), BF16)],
        compiler_params=_cparams(2),
        name="qkv_proj",
    )(x, g, w_qkv)


_SLOPES = np.exp2(-8.0 * np.arange(1, N_HEADS + 1, dtype=np.float32) / N_HEADS)


def _attn_body(q_ref, k_ref, v_ref, o_ref, lse_ref, kcat_ref, vcat_ref, *, dil):
    lb = pl.program_id(1)

    @pl.when(lb == 0)
    def _():
        kcat_ref[0:BLOCK, :] = jnp.zeros((BLOCK, D_MODEL), BF16)
        vcat_ref[0:BLOCK, :] = jnp.zeros((BLOCK, D_MODEL), BF16)

    kcat_ref[BLOCK:2 * BLOCK, :] = k_ref[...]
    vcat_ref[BLOCK:2 * BLOCK, :] = v_ref[...]

    qi = lax.broadcasted_iota(jnp.int32, (BLOCK, 2 * BLOCK), 0)
    ki = lax.broadcasted_iota(jnp.int32, (BLOCK, 2 * BLOCK), 1)
    rel = BLOCK + qi - ki
    valid = (rel >= 0) & (rel <= BLOCK) & ((ki >= BLOCK) | (lb > 0))
    dist = (dil * rel).astype(F32)
    lane = lax.broadcasted_iota(jnp.int32, (BLOCK, LANES), 1)
    lse_tile = jnp.zeros((BLOCK, LANES), F32)

    for h in range(N_HEADS):
        sl = slice(h * HEAD_DIM, (h + 1) * HEAD_DIM)
        s = lax.dot_general(q_ref[:, sl], kcat_ref[:, sl],
                            (((1,), (1,)), ((), ())),
                            preferred_element_type=F32)
        s = jnp.where(valid, s - float(_SLOPES[h]) * dist, NEG_INF)
        mx = jnp.max(s, axis=-1, keepdims=True)
        p = jnp.exp(s - mx)
        den = jnp.sum(p, axis=-1, keepdims=True)
        o = _dot(p.astype(BF16), vcat_ref[:, sl]) / den
        o_ref[:, sl] = o.astype(BF16)
        lse_tile = jnp.where(lane == h, mx + jnp.log(den), lse_tile)

    lse_ref[...] = lse_tile
    kcat_ref[0:BLOCK, :] = k_ref[...]
    vcat_ref[0:BLOCK, :] = v_ref[...]


def _attn_group(qkv, group, dil):
    n_col = qkv.shape[1]
    slabs = n_col // D_MODEL
    seq_l = SEQ // dil
    nb = seq_l // BLOCK
    qkv_v = qkv.reshape(seq_l, dil * n_col)

    def col(which):
        return lambda r, lb: (lb, r * slabs + group * 3 + which)

    o, lse = pl.pallas_call(
        functools.partial(_attn_body, dil=dil),
        grid=(dil, nb),
        in_specs=[
            pl.BlockSpec((BLOCK, D_MODEL), col(0)),
            pl.BlockSpec((BLOCK, D_MODEL), col(1)),
            pl.BlockSpec((BLOCK, D_MODEL), col(2)),
        ],
        out_specs=[
            pl.BlockSpec((BLOCK, D_MODEL), lambda r, lb: (lb, r)),
            pl.BlockSpec((BLOCK, LANES), lambda r, lb: (lb, r)),
        ],
        out_shape=[
            jax.ShapeDtypeStruct((seq_l, dil * D_MODEL), BF16),
            jax.ShapeDtypeStruct((seq_l, dil * LANES), F32),
        ],
        scratch_shapes=[
            pltpu.VMEM((2 * BLOCK, D_MODEL), BF16),
            pltpu.VMEM((2 * BLOCK, D_MODEL), BF16),
        ],
        compiler_params=_cparams(2),
        name=f"attn_d{dil}",
    )(qkv_v, qkv_v, qkv_v)
    return o.reshape(SEQ, D_MODEL), lse.reshape(SEQ, LANES)


def _combine_body(o1_ref, o2_ref, o3_ref, l1_ref, l2_ref, l3_ref, o_ref):
    l1, l2, l3 = l1_ref[...], l2_ref[...], l3_ref[...]
    mx = jnp.maximum(jnp.maximum(l1, l2), l3)
    w1, w2, w3 = jnp.exp(l1 - mx), jnp.exp(l2 - mx), jnp.exp(l3 - mx)
    tot = w1 + w2 + w3
    a1, a2, a3 = w1 / tot, w2 / tot, w3 / tot
    for h in range(N_HEADS):
        sl = slice(h * HEAD_DIM, (h + 1) * HEAD_DIM)
        o = (a1[:, h:h + 1] * o1_ref[:, sl].astype(F32)
             + a2[:, h:h + 1] * o2_ref[:, sl].astype(F32)
             + a3[:, h:h + 1] * o3_ref[:, sl].astype(F32))
        o_ref[:, sl] = o.astype(BF16)


def _combine(os, lses, *, bm):
    ospec = pl.BlockSpec((bm, D_MODEL), lambda m: (m, 0))
    lspec = pl.BlockSpec((bm, LANES), lambda m: (m, 0))
    return pl.pallas_call(
        _combine_body,
        grid=(SEQ // bm,),
        in_specs=[ospec] * 3 + [lspec] * 3,
        out_specs=ospec,
        out_shape=jax.ShapeDtypeStruct((SEQ, D_MODEL), BF16),
        compiler_params=_cparams(1),
        name="attn_combine",
    )(*os, *lses)


def _down_body(a_ref, w_ref, x_ref, o_ref):
    o_ref[...] = x_ref[...] + _dot(a_ref[...], w_ref[...])


def _down(a, w, x, *, bm, bn, name):
    k = a.shape[1]
    return pl.pallas_call(
        _down_body,
        grid=(SEQ // bm, D_MODEL // bn),
        in_specs=[
            pl.BlockSpec((bm, k), lambda m, n: (m, 0)),
            pl.BlockSpec((k, bn), lambda m, n: (0, n)),
            pl.BlockSpec((bm, bn), lambda m, n: (m, n)),
        ],
        out_specs=pl.BlockSpec((bm, bn), lambda m, n: (m, n)),
        out_shape=jax.ShapeDtypeStruct((SEQ, D_MODEL), F32),
        input_output_aliases={2: 0},
        compiler_params=_cparams(2),
        name=name,
    )(a, w, x)


def _final_norm_body(x_ref, g_ref, o_ref):
    o_ref[...] = _rms_tile(x_ref[...], g_ref[...])


def _final_norm(x, g, *, bm):
    return pl.pallas_call(
        _final_norm_body,
        grid=(SEQ // bm,),
        in_specs=[
            pl.BlockSpec((bm, D_MODEL), lambda m: (m, 0)),
            pl.BlockSpec((1, D_MODEL), lambda m: (0, 0)),
        ],
        out_specs=pl.BlockSpec((bm, D_MODEL), lambda m: (m, 0)),
        out_shape=jax.ShapeDtypeStruct((SEQ, D_MODEL), F32),
        compiler_params=_cparams(1),
        name="final_norm",
    )(x, g)


def kernel(x, norm_mix_g, norm_ffn_g, conv_w_in, conv_w_dw, conv_w_out,
           attn_w_qkv, attn_w_o, ffn_w_up, ffn_w_dw, ffn_w_down, final_norm_g):
    xs = x.reshape(SEQ, D_MODEL)
    for i in range(DEPTH):
        j = i // 2
        g_mix = norm_mix_g[i].reshape(1, D_MODEL)
        if i % 2 == 0:
            y = _conv_up(xs, g_mix, conv_w_in[j].astype(BF16), conv_w_dw[j],
                         bm=512, bn=512)
            xs = _down(y, conv_w_out[j].astype(BF16), xs, bm=512, bn=1024,
                       name="conv_down")
        else:
            qkv = _qkv_proj(xs, g_mix, attn_w_qkv[j].astype(BF16), bm=512, bn=1024)
            os, lses = [], []
            for grp, (_, dil) in enumerate(DILATED_GROUPS):
                o, lse = _attn_group(qkv, grp, dil)
                os.append(o)
                lses.append(lse)
            o = _combine(os, lses, bm=512)
            xs = _down(o, attn_w_o[j].astype(BF16), xs, bm=512, bn=1024,
                       name="attn_down")
        g_ffn = norm_ffn_g[i].reshape(1, D_MODEL)
        a = _ffn_up(xs, g_ffn, ffn_w_up[i].astype(BF16), ffn_w_dw[i], bm=512, bn=512)
        xs = _down(a, ffn_w_down[i].astype(BF16), xs, bm=512, bn=512,
                   name="ffn_down")
    out = _final_norm(xs, final_norm_g.reshape(1, D_MODEL), bm=512)
    return out.reshape(x.shape)
```

```python
import functools

import numpy as np
import jax
import jax.numpy as jnp
from jax import lax
from jax.experimental import pallas as pl
from jax.experimental.pallas import tpu as pltpu

D_MODEL = 2048
SEQ = 8192
DEPTH = 4
CONV_WIDTH = 3
HEAD_DIM = 128
N_HEADS = D_MODEL // HEAD_DIM
DILATED_GROUPS = ((128, 1), (512, 4), (2048, 16))
N_GROUPS = len(DILATED_GROUPS)
BLOCK = 128
D_FF = 5632
RMS_EPS = 1e-5
NEG_INF = -1e30

SUBLANES = 8
LANES = 128
VMEM_LIMIT = 56 * 1024 * 1024

F32 = jnp.float32
BF16 = jnp.bfloat16


def _cparams(n_axes):
    return pltpu.CompilerParams(
        dimension_semantics=("arbitrary",) * n_axes,
        vmem_limit_bytes=VMEM_LIMIT)


def _rms_tile(x, g):
    ms = jnp.mean(x * x, axis=-1, keepdims=True)
    return x * lax.rsqrt(ms + RMS_EPS) * g


def _dot(a, b):
    return jnp.dot(a, b, preferred_element_type=F32)


def _causal_conv3(u, buf_ref, carry, dw):
    bm = u.shape[0]
    buf_ref[0:SUBLANES, :] = carry
    buf_ref[SUBLANES:SUBLANES + bm, :] = u
    u1 = buf_ref[SUBLANES - 1:SUBLANES - 1 + bm, :]
    u2 = buf_ref[SUBLANES - 2:SUBLANES - 2 + bm, :]
    return dw[0:1, :] * u2 + dw[1:2, :] * u1 + dw[2:3, :] * u


def _conv_up_body(x_ref, g_ref, wb_ref, wc_ref, wh_ref, dw_ref, y_ref,
                  hn_ref, carry_ref, buf_ref):
    m = pl.program_id(0)
    n = pl.program_id(1)
    bm = x_ref.shape[0]

    @pl.when(n == 0)
    def _():
        hn_ref[...] = _rms_tile(x_ref[...], g_ref[...]).astype(BF16)

    @pl.when(m == 0)
    def _():
        carry_ref[n] = jnp.zeros(carry_ref.shape[1:], F32)

    hb = hn_ref[...]
    b = _dot(hb, wb_ref[...])
    u = _dot(hb, wc_ref[...]) * _dot(hb, wh_ref[...])
    conv = _causal_conv3(u, buf_ref, carry_ref[n], dw_ref[...])
    carry_ref[n] = buf_ref[bm:bm + SUBLANES, :]
    y_ref[...] = (b * conv).astype(BF16)


def _conv_up(x, g, w_in, w_dw, *, bm, bn):
    nt = D_MODEL // bn
    return pl.pallas_call(
        _conv_up_body,
        grid=(SEQ // bm, nt),
        in_specs=[
            pl.BlockSpec((bm, D_MODEL), lambda m, n: (m, 0)),
            pl.BlockSpec((1, D_MODEL), lambda m, n: (0, 0)),
            pl.BlockSpec((D_MODEL, bn), lambda m, n: (0, n)),
            pl.BlockSpec((D_MODEL, bn), lambda m, n: (0, nt + n)),
            pl.BlockSpec((D_MODEL, bn), lambda m, n: (0, 2 * nt + n)),
            pl.BlockSpec((CONV_WIDTH, bn), lambda m, n: (0, n)),
        ],
        out_specs=pl.BlockSpec((bm, bn), lambda m, n: (m, n)),
        out_shape=jax.ShapeDtypeStruct((SEQ, D_MODEL), BF16),
        scratch_shapes=[
            pltpu.VMEM((bm, D_MODEL), BF16),
            pltpu.VMEM((nt, SUBLANES, bn), F32),
            pltpu.VMEM((bm + SUBLANES, bn), F32),
        ],
        compiler_params=_cparams(2),
        name="conv_up",
    )(x, g, w_in, w_in, w_in, w_dw)


def _ffn_up_body(x_ref, g_ref, wg_ref, wu_ref, dwg_ref, dwu_ref, a_ref,
                 hn_ref, carry_ref, gbuf_ref, ubuf_ref):
    m = pl.program_id(0)
    n = pl.program_id(1)
    bm = x_ref.shape[0]

    @pl.when(n == 0)
    def _():
        hn_ref[...] = _rms_tile(x_ref[...], g_ref[...]).astype(BF16)

    @pl.when(m == 0)
    def _():
        carry_ref[n] = jnp.zeros(carry_ref.shape[1:], F32)

    hb = hn_ref[...]
    gz = _dot(hb, wg_ref[...])
    cg = _causal_conv3(gz, gbuf_ref, carry_ref[n, 0], dwg_ref[...])
    carry_ref[n, 0] = gbuf_ref[bm:bm + SUBLANES, :]
    uz = _dot(hb, wu_ref[...])
    cu = _causal_conv3(uz, ubuf_ref, carry_ref[n, 1], dwu_ref[...])
    carry_ref[n, 1] = ubuf_ref[bm:bm + SUBLANES, :]
    act = cg / (1.0 + jnp.exp(-cg)) * cu
    a_ref[...] = act.astype(BF16)


def _ffn_up(x, g, w_up, w_dw, *, bm, bn):
    nt = D_FF // bn
    return pl.pallas_call(
        _ffn_up_body,
        grid=(SEQ // bm, nt),
        in_specs=[
            pl.BlockSpec((bm, D_MODEL), lambda m, n: (m, 0)),
            pl.BlockSpec((1, D_MODEL), lambda m, n: (0, 0)),
            pl.BlockSpec((D_MODEL, bn), lambda m, n: (0, n)),
            pl.BlockSpec((D_MODEL, bn), lambda m, n: (0, nt + n)),
            pl.BlockSpec((CONV_WIDTH, bn), lambda m, n: (0, n)),
            pl.BlockSpec((CONV_WIDTH, bn), lambda m, n: (0, nt + n)),
        ],
        out_specs=pl.BlockSpec((bm, bn), lambda m, n: (m, n)),
        out_shape=jax.ShapeDtypeStruct((SEQ, D_FF), BF16),
        scratch_shapes=[
            pltpu.VMEM((bm, D_MODEL), BF16),
            pltpu.VMEM((nt, 2, SUBLANES, bn), F32),
            pltpu.VMEM((bm + SUBLANES, bn), F32),
            pltpu.VMEM((bm + SUBLANES, bn), F32),
        ],
        compiler_params=_cparams(2),
        name="ffn_up",
    )(x, g, w_up, w_up, w_dw, w_dw)


def _qkv_body(x_ref, g_ref, w_ref, o_ref, hn_ref, acc_ref, *, dil, bn):
    n = pl.program_id(1)
    bm = x_ref.shape[0]

    @pl.when(n == 0)
    def _():
        hn_ref[...] = _rms_tile(x_ref[...], g_ref[...]).astype(BF16)

    scale = jnp.where(n * bn < D_MODEL, HEAD_DIM ** -0.5, 1.0).astype(F32)
    acc = _dot(hn_ref[...], w_ref[...]) * scale
    if dil == 1:
        o_ref[0] = acc.astype(BF16)
    else:
        for c in range(bn // LANES):
            acc_ref[c] = acc[:, c * LANES:(c + 1) * LANES]
        rows = bm // dil
        for r in range(dil):
            for c in range(bn // LANES):
                o_ref[r, :, c * LANES:(c + 1) * LANES] = (
                    acc_ref[c, pl.ds(r, rows, stride=dil), :].astype(BF16))


def _qkv_proj(x, g, w_qkv, group, dil, *, bm, bn):
    n_grp = 3 * D_MODEL
    nt = n_grp // bn
    return pl.pallas_call(
        functools.partial(_qkv_body, dil=dil, bn=bn),
        grid=(SEQ // bm, nt),
        in_specs=[
            pl.BlockSpec((bm, D_MODEL), lambda m, n: (m, 0)),
            pl.BlockSpec((1, D_MODEL), lambda m, n: (0, 0)),
            pl.BlockSpec((D_MODEL, bn), lambda m, n: (0, group * nt + n)),
        ],
        out_specs=pl.BlockSpec((dil, bm // dil, bn), lambda m, n: (0, m, n)),
        out_shape=jax.ShapeDtypeStruct((dil, SEQ // dil, n_grp), BF16),
        scratch_shapes=[
            pltpu.VMEM((bm, D_MODEL), BF16),
            pltpu.VMEM((bn // LANES, bm, LANES), F32),
        ],
        compiler_params=_cparams(2),
        name=f"qkv_proj_d{dil}",
    )(x, g, w_qkv)


_SLOPES = np.exp2(-8.0 * np.arange(1, N_HEADS + 1, dtype=np.float32) / N_HEADS)


def _attn_body(slope_ref, q_ref, k_ref, v_ref, o_ref, lse_ref,
               kcat_ref, vcat_ref, stage_ref, *, dil, nsub, hpc):
    hc = pl.program_id(0)
    lb = pl.program_id(1)
    rows = BLOCK * nsub
    cw = hpc * HEAD_DIM

    @pl.when(lb == 0)
    def _():
        kcat_ref[:, 0:BLOCK, :] = jnp.zeros((dil, BLOCK, cw), BF16)
        vcat_ref[:, 0:BLOCK, :] = jnp.zeros((dil, BLOCK, cw), BF16)

    @pl.when((lb == 0) & (hc == 0))
    def _():
        lse_ref[...] = jnp.zeros(lse_ref.shape, F32)

    kcat_ref[:, BLOCK:BLOCK + rows, :] = k_ref[...]
    vcat_ref[:, BLOCK:BLOCK + rows, :] = v_ref[...]

    qi = lax.broadcasted_iota(jnp.int32, (BLOCK, 2 * BLOCK), 0)
    ki = lax.broadcasted_iota(jnp.int32, (BLOCK, 2 * BLOCK), 1)
    rel = BLOCK + qi - ki
    band = (rel >= 0) & (rel <= BLOCK)
    dist = (dil * rel).astype(F32)
    lane = lax.broadcasted_iota(jnp.int32, (BLOCK, LANES), 1)
    tile_rows = rows * dil
    base = lb * tile_rows

    def one_block(idx, carry):
        if nsub == 1:
            r, row0 = idx, 0
        else:
            r, row0 = 0, pl.multiple_of(idx * BLOCK, BLOCK)
        first = (lb == 0) & (row0 == 0)
        valid = band & (ki >= jnp.where(first, BLOCK, 0))
        if dil == 1:
            out_rows = pl.ds(row0, BLOCK)
            lse_rows = pl.ds(base + row0, BLOCK)
        else:
            out_rows = pl.ds(row0 * dil + r, BLOCK, stride=dil)
            lse_rows = pl.ds(base + row0 * dil + r, BLOCK, stride=dil)
        lse_tile = jnp.zeros((BLOCK, LANES), F32)
        for h in range(hpc):
            sl = slice(h * HEAD_DIM, (h + 1) * HEAD_DIM)
            head = hc * hpc + h
            s = lax.dot_general(q_ref[r, pl.ds(row0, BLOCK), sl],
                                kcat_ref[r, pl.ds(row0, 2 * BLOCK), sl],
                                (((1,), (1,)), ((), ())),
                                preferred_element_type=F32)
            s = jnp.where(valid, s - slope_ref[head] * dist, NEG_INF)
            mx = jnp.max(s, axis=-1, keepdims=True)
            p = jnp.exp(s - mx)
            den = jnp.sum(p, axis=-1, keepdims=True)
            o = _dot(p.astype(BF16), vcat_ref[r, pl.ds(row0, 2 * BLOCK), sl]) / den
            stage_ref[h, out_rows, :] = o
            lse_tile = jnp.where(lane == head, mx + jnp.log(den), lse_tile)
        lse_ref[lse_rows, :] += lse_tile
        return carry

    lax.fori_loop(0, dil * nsub, one_block, 0)
    for h in range(hpc):
        o_ref[:, h * HEAD_DIM:(h + 1) * HEAD_DIM] = stage_ref[h].astype(BF16)
    kcat_ref[:, 0:BLOCK, :] = kcat_ref[:, rows:rows + BLOCK, :]
    vcat_ref[:, 0:BLOCK, :] = vcat_ref[:, rows:rows + BLOCK, :]


def _attn_group(qkv_g, dil, *, nsub, hpc):
    rows = BLOCK * nsub
    cw = hpc * HEAD_DIM
    n_hc = N_HEADS // hpc
    n_lb = SEQ // dil // rows
    tile_rows = rows * dil

    def col(which):
        return lambda hc, lb: (0, lb, which * n_hc + hc)

    return pl.pallas_call(
        functools.partial(_attn_body, dil=dil, nsub=nsub, hpc=hpc),
        grid=(n_hc, n_lb),
        in_specs=[
            pl.BlockSpec(memory_space=pltpu.SMEM),
            pl.BlockSpec((dil, rows, cw), col(0)),
            pl.BlockSpec((dil, rows, cw), col(1)),
            pl.BlockSpec((dil, rows, cw), col(2)),
        ],
        out_specs=[
            pl.BlockSpec((tile_rows, cw), lambda hc, lb: (lb, hc)),
            pl.BlockSpec((SEQ, LANES), lambda hc, lb: (0, 0)),
        ],
        out_shape=[
            jax.ShapeDtypeStruct((SEQ, D_MODEL), BF16),
            jax.ShapeDtypeStruct((SEQ, LANES), F32),
        ],
        scratch_shapes=[
            pltpu.VMEM((dil, BLOCK + rows, cw), BF16),
            pltpu.VMEM((dil, BLOCK + rows, cw), BF16),
            pltpu.VMEM((hpc, tile_rows, HEAD_DIM), F32),
        ],
        compiler_params=_cparams(2),
        name=f"attn_d{dil}",
    )(jnp.asarray(_SLOPES), qkv_g, qkv_g, qkv_g)


def _combine_body(o1_ref, o2_ref, o3_ref, l1_ref, l2_ref, l3_ref, o_ref):
    l1, l2, l3 = l1_ref[...], l2_ref[...], l3_ref[...]
    mx = jnp.maximum(jnp.maximum(l1, l2), l3)
    w1, w2, w3 = jnp.exp(l1 - mx), jnp.exp(l2 - mx), jnp.exp(l3 - mx)
    tot = w1 + w2 + w3
    a1, a2, a3 = w1 / tot, w2 / tot, w3 / tot
    for h in range(N_HEADS):
        sl = slice(h * HEAD_DIM, (h + 1) * HEAD_DIM)
        o = (a1[:, h:h + 1] * o1_ref[:, sl].astype(F32)
             + a2[:, h:h + 1] * o2_ref[:, sl].astype(F32)
             + a3[:, h:h + 1] * o3_ref[:, sl].astype(F32))
        o_ref[:, sl] = o.astype(BF16)


def _combine(os, lses, *, bm):
    ospec = pl.BlockSpec((bm, D_MODEL), lambda m: (m, 0))
    lspec = pl.BlockSpec((bm, LANES), lambda m: (m, 0))
    return pl.pallas_call(
        _combine_body,
        grid=(SEQ // bm,),
        in_specs=[ospec] * 3 + [lspec] * 3,
        out_specs=ospec,
        out_shape=jax.ShapeDtypeStruct((SEQ, D_MODEL), BF16),
        compiler_params=_cparams(1),
        name="attn_combine",
    )(*os, *lses)


def _down_body(a_ref, w_ref, x_ref, o_ref):
    o_ref[...] = x_ref[...] + _dot(a_ref[...], w_ref[...])


def _down(a, w, x, *, bm, bn, name):
    k = a.shape[1]
    return pl.pallas_call(
        _down_body,
        grid=(SEQ // bm, D_MODEL // bn),
        in_specs=[
            pl.BlockSpec((bm, k), lambda m, n: (m, 0)),
            pl.BlockSpec((k, bn), lambda m, n: (0, n)),
            pl.BlockSpec((bm, bn), lambda m, n: (m, n)),
        ],
        out_specs=pl.BlockSpec((bm, bn), lambda m, n: (m, n)),
        out_shape=jax.ShapeDtypeStruct((SEQ, D_MODEL), F32),
        input_output_aliases={2: 0},
        compiler_params=_cparams(2),
        name=name,
    )(a, w, x)


def _final_norm_body(x_ref, g_ref, o_ref):
    o_ref[...] = _rms_tile(x_ref[...], g_ref[...])


def _final_norm(x, g, *, bm):
    return pl.pallas_call(
        _final_norm_body,
        grid=(SEQ // bm,),
        in_specs=[
            pl.BlockSpec((bm, D_MODEL), lambda m: (m, 0)),
            pl.BlockSpec((1, D_MODEL), lambda m: (0, 0)),
        ],
        out_specs=pl.BlockSpec((bm, D_MODEL), lambda m: (m, 0)),
        out_shape=jax.ShapeDtypeStruct((SEQ, D_MODEL), F32),
        compiler_params=_cparams(1),
        name="final_norm",
    )(x, g)


_ATTN_TILING = {1: (4, 16), 4: (1, 16), 16: (1, 4)}


def kernel(x, norm_mix_g, norm_ffn_g, conv_w_in, conv_w_dw, conv_w_out,
           attn_w_qkv, attn_w_o, ffn_w_up, ffn_w_dw, ffn_w_down, final_norm_g):
    xs = x.reshape(SEQ, D_MODEL)
    for i in range(DEPTH):
        j = i // 2
        g_mix = norm_mix_g[i].reshape(1, D_MODEL)
        if i % 2 == 0:
            y = _conv_up(xs, g_mix, conv_w_in[j].astype(BF16), conv_w_dw[j],
                         bm=512, bn=512)
            xs = _down(y, conv_w_out[j].astype(BF16), xs, bm=512, bn=1024,
                       name="conv_down")
        else:
            w_qkv = attn_w_qkv[j].astype(BF16)
            os, lses = [], []
            for grp, (_, dil) in enumerate(DILATED_GROUPS):
                qkv_g = _qkv_proj(xs, g_mix, w_qkv, grp, dil, bm=512, bn=1024)
                nsub, hpc = _ATTN_TILING[dil]
                o, lse = _attn_group(qkv_g, dil, nsub=nsub, hpc=hpc)
                os.append(o)
                lses.append(lse)
            o = _combine(os, lses, bm=512)
            xs = _down(o, attn_w_o[j].astype(BF16), xs, bm=512, bn=1024,
                       name="attn_down")
        g_ffn = norm_ffn_g[i].reshape(1, D_MODEL)
        a = _ffn_up(xs, g_ffn, ffn_w_up[i].astype(BF16), ffn_w_dw[i], bm=512, bn=512)
        xs = _down(a, ffn_w_down[i].astype(BF16), xs, bm=512, bn=512,
                   name="ffn_down")
    out = _final_norm(xs, final_norm_g.reshape(1, D_MODEL), bm=512)
    return out.reshape(x.shape)
```
